```python
import math
import jax
import jax.numpy as jnp
from jax import lax
import numpy as np

D_MODEL = 2048
BATCH = 2
SEQ = 4096
DEPTH = 4

N_MIXERS = 2
N_CONV_LAYERS = (DEPTH + 1) // 2
N_NSA_LAYERS = DEPTH // 2
CONV_WIDTH = 3
NSA_HEADS = 16
NSA_HEAD_DIM = 128
NSA_KV_GROUPS = 4
NSA_HPG = NSA_HEADS // NSA_KV_GROUPS
NSA_Q_COLS = NSA_HEADS * NSA_HEAD_DIM
NSA_KV_COLS = NSA_KV_GROUPS * NSA_HEAD_DIM
NSA_N_KV = 6
NSA_GATE_COLS = 3 * NSA_HEADS
NSA_IN_COLS = NSA_Q_COLS + NSA_N_KV * NSA_KV_COLS + NSA_GATE_COLS
CMP_BLOCK = 32
CMP_STRIDE = 16
CMP_HIDDEN = 256
SEL_BLOCK = 64
SEL_TOPK = 16
WINDOW = 512
Q_BLOCK = 128
FORCE_SCORE = 1e4
NEG_INF = -1e30
REL_BUCKETS = 32
REL_MAX_DIST = 128
PEER_HEADS = 8
PEER_NKEYS = 128
PEER_EXPERTS = PEER_NKEYS ** 2
PEER_DKEY = 256
PEER_TOPK = 16
PEER_CHUNK = 128
NORM_EPS = 1e-6

kernel_name = 'hybrid_shortconv_nsa_peer'


def rmsnorm(x, g):
    xf = x.astype(jnp.float32)
    y = xf * lax.rsqrt(jnp.mean(xf * xf, axis=-1, keepdims=True) + NORM_EPS)
    return (y * g.astype(jnp.float32)).astype(x.dtype)


def masked_softmax(logits, valid):
    z = jnp.where(valid, logits.astype(jnp.float32), NEG_INF)
    return jax.nn.softmax(z, axis=-1) * valid


def t5_bucket(dist):
    n = jnp.maximum(dist, 0)
    max_exact = REL_BUCKETS // 2
    nf = jnp.maximum(n, 1).astype(jnp.float32)
    large = max_exact + (jnp.log(nf / max_exact) / math.log(REL_MAX_DIST / max_exact)
                         * (REL_BUCKETS - max_exact)).astype(jnp.int32)
    large = jnp.minimum(large, REL_BUCKETS - 1)
    return jnp.where(n < max_exact, n, large)


def shortconv_mixer(h, w_in, conv_k, w_out):
    D = h.shape[-1]
    b_gate, c_gate, u = jnp.split(h @ w_in, 3, axis=-1)
    z = c_gate * u
    zc = lax.conv_general_dilated(z, conv_k[:, None, :], window_strides=(1,),
                                  padding=[(CONV_WIDTH - 1, 0)],
                                  dimension_numbers=('NWC', 'WIO', 'NWC'),
                                  feature_group_count=D)
    return (b_gate * zc) @ w_out


def compress_blocks(tok, blk_idx, pos, w1, w2):
    B, G, _, dh = tok.shape
    blocks = tok[:, :, blk_idx] + pos
    flat = blocks.reshape(B, G, blk_idx.shape[0], CMP_BLOCK * dh)
    return jax.nn.gelu(flat @ w1) @ w2


def nsa_mixer(h, w_in, cmp_pos_k, cmp_pos_v, cmp_w1_k, cmp_w2_k, cmp_w1_v, cmp_w2_v, w_out, rel_bias):
    B, S, _ = h.shape
    G, R, dh = NSA_KV_GROUPS, NSA_HPG, NSA_HEAD_DIM
    n_c = (S - CMP_BLOCK) // CMP_STRIDE + 1
    n_sel = S // SEL_BLOCK
    k_top = min(SEL_TOPK, n_sel)
    nq = S // Q_BLOCK
    scale = dh ** -0.5

    proj = h @ w_in
    q = proj[..., :NSA_Q_COLS].reshape(B, S, G, R, dh).transpose(0, 2, 3, 1, 4)
    kv = proj[..., NSA_Q_COLS:NSA_Q_COLS + NSA_N_KV * NSA_KV_COLS]
    kv = kv.reshape(B, S, NSA_N_KV, G, dh).transpose(2, 0, 3, 1, 4)
    k_cmp_tok, v_cmp_tok, k_sel, v_sel, k_win, v_win = kv[0], kv[1], kv[2], kv[3], kv[4], kv[5]
    gate_logits = proj[..., NSA_Q_COLS + NSA_N_KV * NSA_KV_COLS:].astype(jnp.float32)
    gates = jax.nn.sigmoid(gate_logits).reshape(B, S, 3, G, R).transpose(2, 0, 3, 4, 1)[..., None]

    blk_idx = jnp.arange(n_c)[:, None] * CMP_STRIDE + jnp.arange(CMP_BLOCK)[None]
    k_c = compress_blocks(k_cmp_tok, blk_idx, cmp_pos_k, cmp_w1_k, cmp_w2_k)
    v_c = compress_blocks(v_cmp_tok, blk_idx, cmp_pos_v, cmp_w1_v, cmp_w2_v)
    c_start = jnp.arange(n_c) * CMP_STRIDE
    c_end = c_start + CMP_BLOCK - 1
    j_sel = jnp.arange(n_sel)
    overlap = ((c_start[:, None] < (j_sel[None] + 1) * SEL_BLOCK)
               & (j_sel[None] * SEL_BLOCK <= c_end[:, None])).astype(jnp.float32)

    k_blocks = k_sel.reshape(B, G, n_sel, SEL_BLOCK, dh)
    v_blocks = v_sel.reshape(B, G, n_sel, SEL_BLOCK, dh)
    kw_pad = jnp.pad(k_win, ((0, 0), (0, 0), (WINDOW, 0), (0, 0)))
    vw_pad = jnp.pad(v_win, ((0, 0), (0, 0), (WINDOW, 0), (0, 0)))
    b_ix = jnp.arange(B)[:, None, None, None]
    g_ix = jnp.arange(G)[None, :, None, None]
    rb_heads = rel_bias.reshape(REL_BUCKETS, G, R)

    def head_bias(dist):
        return jnp.moveaxis(rb_heads[t5_bucket(dist)], (2, 3), (0, 1))

    def block(args):
        q_blk, i = args
        t = i * Q_BLOCK + jnp.arange(Q_BLOCK)
        dist_c = t[:, None] - c_end[None]
        lc = jnp.einsum('bgrqd,bgcd->bgrqc', q_blk, k_c) * scale + head_bias(dist_c)
        p_c = masked_softmax(lc, dist_c >= 0)
        o_c = jnp.einsum('bgrqc,bgcd->bgrqd', p_c.astype(v_c.dtype), v_c)
        imp = jnp.einsum('bgrqc,cj->bgqj', p_c, overlap)
        cur = t // SEL_BLOCK
        forced = (j_sel[None] == 0) | (j_sel[None] == cur[:, None]) | (j_sel[None] == cur[:, None] - 1)
        causal_j = j_sel[None] * SEL_BLOCK <= t[:, None]
        score = jnp.where(forced, FORCE_SCORE, jnp.where(causal_j, imp, -1.0))
        _, sel = lax.top_k(score, k_top)
        kg = k_blocks[b_ix, g_ix, sel]
        vg = v_blocks[b_ix, g_ix, sel]
        pos_s = sel[..., None] * SEL_BLOCK + jnp.arange(SEL_BLOCK)
        dist_s = t[:, None, None] - pos_s
        bias_s = jnp.moveaxis(rb_heads[t5_bucket(dist_s), g_ix[..., None]], -1, 2)
        ls = jnp.einsum('bgrqd,bgqkld->bgrqkl', q_blk, kg) * scale + bias_s
        p_s = masked_softmax(ls.reshape(B, G, R, Q_BLOCK, k_top * SEL_BLOCK),
                             (dist_s >= 0).reshape(B, G, 1, Q_BLOCK, k_top * SEL_BLOCK))
        p_s = p_s.reshape(B, G, R, Q_BLOCK, k_top, SEL_BLOCK)
        o_s = jnp.einsum('bgrqkl,bgqkld->bgrqd', p_s.astype(vg.dtype), vg)
        kw = lax.dynamic_slice_in_dim(kw_pad, i * Q_BLOCK, Q_BLOCK + WINDOW, axis=2)
        vw = lax.dynamic_slice_in_dim(vw_pad, i * Q_BLOCK, Q_BLOCK + WINDOW, axis=2)
        pos_w = i * Q_BLOCK - WINDOW + jnp.arange(Q_BLOCK + WINDOW)
        dist_w = t[:, None] - pos_w[None]
        valid_w = (dist_w >= 0) & (dist_w < WINDOW) & (pos_w[None] >= 0)
        lw = jnp.einsum('bgrqd,bgkd->bgrqk', q_blk, kw) * scale + head_bias(dist_w)
        p_w = masked_softmax(lw, valid_w)
        o_w = jnp.einsum('bgrqk,bgkd->bgrqd', p_w.astype(vw.dtype), vw)
        return o_c, o_s, o_w

    q_blocks = jnp.moveaxis(q.reshape(B, G, R, nq, Q_BLOCK, dh), 3, 0)
    o_c, o_s, o_w = lax.map(block, (q_blocks, jnp.arange(nq)))

    def unblock(o):
        return jnp.moveaxis(o, 0, 3).reshape(B, G, R, S, dh)

    o = gates[0] * unblock(o_c) + gates[1] * unblock(o_s) + gates[2] * unblock(o_w)
    o = o.transpose(0, 3, 1, 2, 4).reshape(B, S, NSA_Q_COLS).astype(h.dtype)
    return o @ w_out


def peer_ffn(h, w_q, sub_keys, u_emb, v_emb):
    B, S, D = h.shape
    T = B * S
    xt = h.reshape(T, D)
    q = (xt @ w_q).reshape(T, PEER_HEADS, 2, PEER_DKEY // 2)
    s = jnp.einsum('thpd,hpnd->thpn', q, sub_keys).astype(jnp.float32)
    s1, i1 = lax.top_k(s[:, :, 0], PEER_TOPK)
    s2, i2 = lax.top_k(s[:, :, 1], PEER_TOPK)
    cand = (s1[..., :, None] + s2[..., None, :]).reshape(T, PEER_HEADS, PEER_TOPK * PEER_TOPK)
    cand_idx = (i1[..., :, None] * PEER_NKEYS + i2[..., None, :]).reshape(T, PEER_HEADS, PEER_TOPK * PEER_TOPK)
    top_s, top_pos = lax.top_k(cand, PEER_TOPK)
    idx = jnp.take_along_axis(cand_idx, top_pos, axis=-1).reshape(T, PEER_HEADS * PEER_TOPK)
    gate = jax.nn.softmax(top_s, axis=-1).reshape(T, PEER_HEADS * PEER_TOPK).astype(h.dtype)
    n_chunk = T // PEER_CHUNK

    def chunk(args):
        x_c, idx_c, g_c = args
        act = jax.nn.gelu(jnp.einsum('ckd,cd->ck', u_emb[idx_c], x_c))
        return jnp.einsum('ck,ckd->cd', g_c * act, v_emb[idx_c])

    out = lax.map(chunk, (xt.reshape(n_chunk, PEER_CHUNK, D),
                          idx.reshape(n_chunk, PEER_CHUNK, -1),
                          gate.reshape(n_chunk, PEER_CHUNK, -1)))
    return out.reshape(B, S, D)


def setup_inputs(seed: int = 0) -> dict:
    key = jax.random.key(seed)
    ks = jax.random.split(key, 22)
    D = D_MODEL
    dh = NSA_HEAD_DIM

    def nrm(k, shape, scale):
        return jax.random.normal(k, shape, jnp.float32) * scale

    return {
        'x': nrm(ks[0], (BATCH, SEQ, D), 1.0),
        'rel_bias': nrm(ks[1], (REL_BUCKETS, NSA_HEADS), 0.5),
        'mix_norm': 1.0 + nrm(ks[2], (DEPTH, D), 0.01),
        'ffn_norm': 1.0 + nrm(ks[3], (DEPTH, D), 0.01),
        'final_norm': 1.0 + nrm(ks[4], (D,), 0.01),
        'conv_w_in': nrm(ks[5], (N_CONV_LAYERS, D, 3 * D), D ** -0.5),
        'conv_kernel': nrm(ks[6], (N_CONV_LAYERS, CONV_WIDTH, D), CONV_WIDTH ** -0.5),
        'conv_w_out': nrm(ks[7], (N_CONV_LAYERS, D, D), D ** -0.5),
        'nsa_w_in': nrm(ks[8], (N_NSA_LAYERS, D, NSA_IN_COLS), D ** -0.5),
        'nsa_cmp_pos_k': nrm(ks[9], (N_NSA_LAYERS, CMP_BLOCK, dh), 0.02),
        'nsa_cmp_pos_v': nrm(ks[10], (N_NSA_LAYERS, CMP_BLOCK, dh), 0.02),
        'nsa_cmp_w1_k': nrm(ks[11], (N_NSA_LAYERS, CMP_BLOCK * dh, CMP_HIDDEN), (CMP_BLOCK * dh) ** -0.5),
        'nsa_cmp_w2_k': nrm(ks[12], (N_NSA_LAYERS, CMP_HIDDEN, dh), CMP_HIDDEN ** -0.5),
        'nsa_cmp_w1_v': nrm(ks[13], (N_NSA_LAYERS, CMP_BLOCK * dh, CMP_HIDDEN), (CMP_BLOCK * dh) ** -0.5),
        'nsa_cmp_w2_v': nrm(ks[14], (N_NSA_LAYERS, CMP_HIDDEN, dh), CMP_HIDDEN ** -0.5),
        'nsa_w_out': nrm(ks[15], (N_NSA_LAYERS, NSA_Q_COLS, D), NSA_Q_COLS ** -0.5),
        'peer_w_q': nrm(ks[16], (DEPTH, D, PEER_HEADS * PEER_DKEY), D ** -0.5),
        'peer_sub_keys': nrm(ks[17], (DEPTH, PEER_HEADS, 2, PEER_NKEYS, PEER_DKEY // 2), (PEER_DKEY // 2) ** -0.5),
        'peer_u': nrm(ks[18], (DEPTH, PEER_EXPERTS, D), D ** -0.5),
        'peer_v': nrm(ks[19], (DEPTH, PEER_EXPERTS, D), (PEER_HEADS * PEER_TOPK) ** -0.5),
    }


def reference(x, rel_bias, mix_norm, ffn_norm, final_norm, conv_w_in, conv_kernel, conv_w_out,
              nsa_w_in, nsa_cmp_pos_k, nsa_cmp_pos_v, nsa_cmp_w1_k, nsa_cmp_w2_k, nsa_cmp_w1_v,
              nsa_cmp_w2_v, nsa_w_out, peer_w_q, peer_sub_keys, peer_u, peer_v):
    h = x
    for i in range(DEPTH):
        hn = rmsnorm(h, mix_norm[i])
        j = i // N_MIXERS
        if i % N_MIXERS == 0:
            h = h + shortconv_mixer(hn, conv_w_in[j], conv_kernel[j], conv_w_out[j])
        else:
            h = h + nsa_mixer(hn, nsa_w_in[j], nsa_cmp_pos_k[j], nsa_cmp_pos_v[j],
                              nsa_cmp_w1_k[j], nsa_cmp_w2_k[j], nsa_cmp_w1_v[j], nsa_cmp_w2_v[j],
                              nsa_w_out[j], rel_bias)
        h = h + peer_ffn(rmsnorm(h, ffn_norm[i]), peer_w_q[i], peer_sub_keys[i], peer_u[i], peer_v[i])
    return rmsnorm(h, final_norm)
```

```python
import functools
import math

import numpy as np
import jax
import jax.numpy as jnp
from jax import lax
from jax.experimental import pallas as pl
from jax.experimental.pallas import tpu as pltpu

F32 = jnp.float32
MXU_DTYPE = jnp.bfloat16

NSA_HEADS = 16
HEAD_DIM = 128
KV_GROUPS = 4
HPG = NSA_HEADS // KV_GROUPS
Q_COLS = NSA_HEADS * HEAD_DIM
KV_COLS = KV_GROUPS * HEAD_DIM
N_KV = 6
GATE_COLS = 3 * NSA_HEADS
CMP_BLOCK = 32
CMP_STRIDE = 16
CMP_HIDDEN = 256
SEL_BLOCK = 64
SEL_TOPK = 16
WINDOW = 512
Q_BLOCK = 128
FORCE_SCORE = 1e4
NEG_INF = -1e30
REL_BUCKETS = 32
REL_MAX_DIST = 128
PEER_HEADS = 8
PEER_NKEYS = 128
PEER_DKEY = 256
PEER_TOPK = 16
NORM_EPS = 1e-6

LANES = 128
SUBLANES = 8
VMEM_LIMIT_BYTES = 48 * 1024 * 1024


def _cparams(*sem):
    return pltpu.CompilerParams(dimension_semantics=sem, vmem_limit_bytes=VMEM_LIMIT_BYTES)


def _dot(a, b):
    return jnp.dot(a, b, preferred_element_type=F32)


def _dot_nt(a, b):
    return lax.dot_general(a, b, (((1,), (1,)), ((), ())), preferred_element_type=F32)


def _split3(x):
    p1 = x.astype(MXU_DTYPE)
    r1 = x - p1.astype(F32)
    p2 = r1.astype(MXU_DTYPE)
    p3 = (r1 - p2.astype(F32)).astype(MXU_DTYPE)
    return p1, p2, p3


def _rmsnorm_kernel(x_ref, g_ref, o_ref):
    x = x_ref[...].astype(F32)
    ms = jnp.mean(x * x, axis=-1, keepdims=True)
    o_ref[...] = (x * lax.rsqrt(ms + NORM_EPS) * g_ref[...]).astype(o_ref.dtype)


def _rmsnorm(x, g, out_dtype, tm=256):
    t, d = x.shape
    return pl.pallas_call(
        _rmsnorm_kernel,
        grid=(t // tm,),
        in_specs=[pl.BlockSpec((tm, d), lambda i: (i, 0)), pl.BlockSpec((1, d), lambda i: (0, 0))],
        out_specs=pl.BlockSpec((tm, d), lambda i: (i, 0)),
        out_shape=jax.ShapeDtypeStruct((t, d), out_dtype),
        compiler_params=_cparams("arbitrary"),
        name="rmsnorm",
    )(x, g.reshape(1, d).astype(F32))


def _mm_kernel(*refs, act, has_res):
    if has_res:
        x_ref, w_ref, r_ref, o_ref = refs
    else:
        x_ref, w_ref, o_ref = refs
    acc = _dot(x_ref[...], w_ref[...])
    if act == "sigmoid":
        acc = jax.nn.sigmoid(acc)
    if has_res:
        acc = acc + r_ref[...]
    o_ref[...] = acc.astype(o_ref.dtype)


def _matmul(x, w, out_dtype, *, res=None, act=None, tm=512, tn=512, name="matmul"):
    t, k = x.shape
    n = w.shape[1]
    tm, tn = min(tm, t), min(tn, n)
    in_specs = [pl.BlockSpec((tm, k), lambda j, i: (i, 0)), pl.BlockSpec((k, tn), lambda j, i: (0, j))]
    args = [x, w]
    if res is not None:
        in_specs.append(pl.BlockSpec((tm, tn), lambda j, i: (i, j)))
        args.append(res)
    return pl.pallas_call(
        functools.partial(_mm_kernel, act=act, has_res=res is not None),
        grid=(n // tn, t // tm),
        in_specs=in_specs,
        out_specs=pl.BlockSpec((tm, tn), lambda j, i: (i, j)),
        out_shape=jax.ShapeDtypeStruct((t, n), out_dtype),
        compiler_params=_cparams("arbitrary", "arbitrary"),
        name=name,
    )(*args)


def _conv_in_kernel(x_ref, wb_ref, wc_ref, wu_ref, ck_ref, y_ref, halo_ref, *, tiles_per_seq):
    i = pl.program_id(1)

    @pl.when(i % tiles_per_seq == 0)
    def _():
        halo_ref[...] = jnp.zeros_like(halo_ref)

    x = x_ref[...]
    b = _dot(x, wb_ref[...])
    z = _dot(x, wc_ref[...]) * _dot(x, wu_ref[...])
    tm = z.shape[0]
    row = lax.broadcasted_iota(jnp.int32, z.shape, 0)
    prev1 = halo_ref[SUBLANES - 1:SUBLANES, :]
    prev2 = halo_ref[SUBLANES - 2:SUBLANES - 1, :]
    z1 = jnp.where(row == 0, prev1, pltpu.roll(z, 1, 0))
    z2 = jnp.where(row == 0, prev2, jnp.where(row == 1, prev1, pltpu.roll(z, 2, 0)))
    ck = ck_ref[...]
    zc = ck[2:3, :] * z + ck[1:2, :] * z1 + ck[0:1, :] * z2
    y_ref[...] = (b * zc).astype(y_ref.dtype)
    halo_ref[...] = z[tm - SUBLANES:tm, :]


def _conv_in(hn, w_in, conv_k, seq, tm=512, tn=512):
    t, d = hn.shape
    nb = d // tn
    return pl.pallas_call(
        functools.partial(_conv_in_kernel, tiles_per_seq=seq // tm),
        grid=(nb, t // tm),
        in_specs=[
            pl.BlockSpec((tm, d), lambda j, i: (i, 0)),
            pl.BlockSpec((d, tn), lambda j, i: (0, j)),
            pl.BlockSpec((d, tn), lambda j, i: (0, nb + j)),
            pl.BlockSpec((d, tn), lambda j, i: (0, 2 * nb + j)),
            pl.BlockSpec((3, tn), lambda j, i: (0, j)),
        ],
        out_specs=pl.BlockSpec((tm, tn), lambda j, i: (i, j)),
        out_shape=jax.ShapeDtypeStruct((t, d), MXU_DTYPE),
        scratch_shapes=[pltpu.VMEM((SUBLANES, tn), F32)],
        compiler_params=_cparams("arbitrary", "arbitrary"),
        name="conv_in",
    )(hn, w_in, w_in, w_in, conv_k)


def _shortconv_layer(h2, seq, norm_g, w_in, conv_k, w_out):
    hn = _rmsnorm(h2, norm_g, MXU_DTYPE)
    y = _conv_in(hn, w_in.astype(MXU_DTYPE), conv_k.astype(F32), seq)
    return _matmul(y, w_out.astype(MXU_DTYPE), F32, res=h2, name="conv_out")


def _t5_bucket_table(n_max):
    n = np.arange(n_max)
    max_exact = REL_BUCKETS // 2
    nf = np.maximum(n, 1).astype(np.float32)
    large = max_exact + (np.log(nf / np.float32(max_exact)) / np.float32(math.log(REL_MAX_DIST / max_exact))
                         * np.float32(REL_BUCKETS - max_exact)).astype(np.int32)
    large = np.minimum(large, REL_BUCKETS - 1)
    return np.where(n < max_exact, n, large)


def _nsa_compress_kernel(tok_ref, pos_ref, w1_ref, w2_ref, o_ref, tokf_ref):
    nh = o_ref.shape[-2]
    tokf_ref[...] = tok_ref[0].astype(F32)
    half = CMP_BLOCK // 2
    lo = jnp.zeros((nh, CMP_HIDDEN), F32)
    hi = jnp.zeros((nh, CMP_HIDDEN), F32)
    for i in range(half):
        x = tokf_ref[pl.ds(i, nh, stride=CMP_STRIDE), :]
        xl = (x + pos_ref[0, i:i + 1, :]).astype(MXU_DTYPE)
        xh = (x + pos_ref[0, half + i:half + i + 1, :]).astype(MXU_DTYPE)
        lo = lo + _dot(xl, w1_ref[0, i * HEAD_DIM:(i + 1) * HEAD_DIM, :])
        hi = hi + _dot(xh, w1_ref[0, (half + i) * HEAD_DIM:(half + i + 1) * HEAD_DIM, :])
    pre = lo + pltpu.roll(hi, nh - 1, 0)
    hid = jax.nn.gelu(pre).astype(MXU_DTYPE)
    o_ref[0, 0, 0] = _dot(hid, w2_ref[0]).astype(o_ref.dtype)


def _nsa_compress(proj3, pos, w1, w2):
    b, s, _ = proj3.shape
    nh = s // CMP_STRIDE
    first_kv_block = Q_COLS // HEAD_DIM
    return pl.pallas_call(
        _nsa_compress_kernel,
        grid=(b, 2, KV_GROUPS),
        in_specs=[
            pl.BlockSpec((1, s, HEAD_DIM), lambda bi, n, g: (bi, 0, first_kv_block + n * KV_GROUPS + g)),
            pl.BlockSpec((1, CMP_BLOCK, HEAD_DIM), lambda bi, n, g: (n, 0, 0)),
            pl.BlockSpec((1, CMP_BLOCK * HEAD_DIM, CMP_HIDDEN), lambda bi, n, g: (n, 0, 0)),
            pl.BlockSpec((1, CMP_HIDDEN, HEAD_DIM), lambda bi, n, g: (n, 0, 0)),
        ],
        out_specs=pl.BlockSpec((1, 1, 1, nh, HEAD_DIM), lambda bi, n, g: (bi, n, g, 0, 0)),
        out_shape=jax.ShapeDtypeStruct((b, 2, KV_GROUPS, nh, HEAD_DIM), MXU_DTYPE),
        scratch_shapes=[pltpu.VMEM((s, HEAD_DIM), F32)],
        compiler_params=_cparams("arbitrary", "arbitrary", "arbitrary"),
        name="nsa_compress",
    )(proj3, pos, w1, w2)


def _nsa_attn_kernel(q_ref, ksel_ref, vsel_ref, kwin_ref, vwin_ref, kc_ref, vc_ref, gate_ref, toep_ref, fd_ref,
                     o_ref, *, seq, k_top):
    i = pl.program_id(2)
    nh = kc_ref.shape[-2]
    n_sel = seq // SEL_BLOCK
    scale = HEAD_DIM ** -0.5
    qb = Q_BLOCK

    q = q_ref[0]
    qs = jnp.concatenate([q[:, r * HEAD_DIM:(r + 1) * HEAD_DIM] for r in range(HPG)], axis=0)

    def rep_heads(x):
        return jnp.concatenate([x] * HPG, axis=0)

    tq_c = lax.broadcasted_iota(jnp.int32, (qb, nh), 0)
    cc = lax.broadcasted_iota(jnp.int32, (qb, nh), 1)
    dist_c = i * qb + tq_c - (cc * CMP_STRIDE + CMP_BLOCK - 1)
    valid_c = rep_heads(dist_c >= 0)
    onehot = ((i * (qb // CMP_STRIDE) - cc + 7) == tq_c).astype(MXU_DTYPE)
    fd = fd_ref[...].reshape(HPG * qb, LANES)
    fd_hi = fd.astype(MXU_DTYPE)
    fd_lo = (fd - fd_hi.astype(F32)).astype(MXU_DTYPE)
    far = jnp.concatenate([jnp.concatenate([toep_ref[r, 2]] * (nh // LANES), axis=1) for r in range(HPG)], axis=0)
    bias_c = far + _dot(fd_hi, onehot) + _dot(fd_lo, onehot)
    lc = _dot_nt(qs, kc_ref[0, 0, 0]) * scale + bias_c
    zc = jnp.where(valid_c, lc, NEG_INF)
    mc = jnp.max(zc, axis=-1, keepdims=True)
    ec = jnp.where(valid_c, jnp.exp(zc - mc), 0.0)
    sc = jnp.sum(ec, axis=-1, keepdims=True)
    pc = ec * (1.0 / jnp.where(sc > 0.0, sc, 1.0))
    o_c = _dot(pc.astype(MXU_DTYPE), vc_ref[0, 0, 0])

    psum = pc[0:qb] + pc[qb:2 * qb] + pc[2 * qb:3 * qb] + pc[3 * qb:4 * qb]
    jo = lax.broadcasted_iota(jnp.int32, (n_sel, nh), 0)
    co = lax.broadcasted_iota(jnp.int32, (n_sel, nh), 1)
    ratio = SEL_BLOCK // CMP_STRIDE
    overlap_t = ((co >= ratio * jo - 1) & (co <= ratio * jo + ratio - 1)).astype(MXU_DTYPE)
    imp_t = sum(_dot_nt(overlap_t, part) for part in _split3(psum))
    jj = lax.broadcasted_iota(jnp.int32, (n_sel, qb), 0)
    tl = i * qb + lax.broadcasted_iota(jnp.int32, (n_sel, qb), 1)
    cur = lax.shift_right_logical(tl, int(math.log2(SEL_BLOCK)))
    forced = (jj == 0) | (jj == cur) | (jj == cur - 1)
    score = jnp.where(forced, FORCE_SCORE, jnp.where(jj * SEL_BLOCK <= tl, imp_t, -1.0))
    rank = jnp.zeros((n_sel, qb), F32)
    for j in range(n_sel):
        row = score[j:j + 1, :]
        ahead = (row > score) | ((row == score) & (jj > j))
        rank = rank + jnp.where(ahead, 1.0, 0.0)
    sel_t = jnp.where(rank < k_top, 1.0, 0.0)
    if n_sel < LANES:
        sel_t = jnp.concatenate([sel_t, jnp.zeros((LANES - n_sel, qb), F32)], axis=0)
    eye = (lax.broadcasted_iota(jnp.int32, (qb, qb), 0) == lax.broadcasted_iota(jnp.int32, (qb, qb), 1))
    sel = _dot_nt(eye.astype(MXU_DTYPE), sel_t.astype(MXU_DTYPE)).astype(MXU_DTYPE)

    tq = lax.broadcasted_iota(jnp.int32, (qb, qb), 0)
    tk = lax.broadcasted_iota(jnp.int32, (qb, qb), 1)

    def attend(k_ref, v_ref, lo, hi, selected):
        def body(kt, carry):
            m, l, acc = carry
            off = pl.multiple_of(kt * qb, qb)
            k = k_ref[0, pl.ds(off, qb), :]
            v = v_ref[0, pl.ds(off, qb), :]
            d = jnp.minimum(i - kt, 2)
            bias = jnp.concatenate([toep_ref[r, d] for r in range(HPG)], axis=0)
            s = _dot_nt(qs, k) * scale + bias
            dist = (i - kt) * qb + tq - tk
            if selected:
                expand = (tq == lax.shift_right_logical(kt * qb + tk, int(math.log2(SEL_BLOCK)))).astype(MXU_DTYPE)
                ok = (dist >= 0) & (_dot(sel, expand) > 0.5)
            else:
                ok = (dist >= 0) & (dist < WINDOW)
            ok = rep_heads(ok)
            z = jnp.where(ok, s, NEG_INF)
            m_new = jnp.maximum(m, jnp.max(z, axis=-1, keepdims=True))
            alpha = jnp.exp(m - m_new)
            p = jnp.where(ok, jnp.exp(z - m_new), 0.0)
            l = alpha * l + jnp.sum(p, axis=-1, keepdims=True)
            acc = alpha * acc + _dot(p.astype(MXU_DTYPE), v)
            return m_new, l, acc

        init = (jnp.full((HPG * qb, 1), NEG_INF, F32), jnp.zeros((HPG * qb, 1), F32),
                jnp.zeros((HPG * qb, HEAD_DIM), F32))
        _, l, acc = lax.fori_loop(lo, hi, body, init)
        return acc * (1.0 / jnp.where(l > 0.0, l, 1.0))

    o_s = attend(ksel_ref, vsel_ref, 0, i + 1, True)
    o_w = attend(kwin_ref, vwin_ref, jnp.maximum(i - WINDOW // qb, 0), i + 1, False)

    g = gate_ref[0]
    for r in range(HPG):
        rows = slice(r * qb, (r + 1) * qb)
        o_r = (g[:, r:r + 1] * o_c[rows] + g[:, HPG + r:HPG + r + 1] * o_s[rows]
               + g[:, 2 * HPG + r:2 * HPG + r + 1] * o_w[rows])
        o_ref[0, :, r * HEAD_DIM:(r + 1) * HEAD_DIM] = o_r.astype(o_ref.dtype)


def _nsa_attention(proj3, kvc, gates3, toep, fd):
    b, s, _ = proj3.shape
    nh = s // CMP_STRIDE
    nq = s // Q_BLOCK
    k_top = min(SEL_TOPK, s // SEL_BLOCK)
    first_kv_block = Q_COLS // HEAD_DIM

    def kv_spec(n):
        return pl.BlockSpec((1, s, HEAD_DIM), lambda bi, g, i: (bi, 0, first_kv_block + n * KV_GROUPS + g))

    return pl.pallas_call(
        functools.partial(_nsa_attn_kernel, seq=s, k_top=k_top),
        grid=(b, KV_GROUPS, nq),
        in_specs=[
            pl.BlockSpec((1, Q_BLOCK, HPG * HEAD_DIM), lambda bi, g, i: (bi, i, g)),
            kv_spec(2), kv_spec(3), kv_spec(4), kv_spec(5),
            pl.BlockSpec((1, 1, 1, nh, HEAD_DIM), lambda bi, g, i: (bi, 0, g, 0, 0)),
            pl.BlockSpec((1, 1, 1, nh, HEAD_DIM), lambda bi, g, i: (bi, 1, g, 0, 0)),
            pl.BlockSpec((1, Q_BLOCK, LANES), lambda bi, g, i: (bi, i, g)),
            pl.BlockSpec((HPG, 3, Q_BLOCK, Q_BLOCK), lambda bi, g, i: (g, 0, 0, 0)),
            pl.BlockSpec((HPG, Q_BLOCK, LANES), lambda bi, g, i: (g, 0, 0)),
        ],
        out_specs=pl.BlockSpec((1, Q_BLOCK, HPG * HEAD_DIM), lambda bi, g, i: (bi, i, g)),
        out_shape=jax.ShapeDtypeStruct((b, s, Q_COLS), MXU_DTYPE),
        compiler_params=_cparams("arbitrary", "arbitrary", "arbitrary"),
        name="nsa_attention",
    )(proj3, proj3, proj3, proj3, proj3, kvc, kvc, gates3, toep, fd)


def _nsa_bias_tables(rel_bias):
    n_tab = 2 * Q_BLOCK
    tab = rel_bias.astype(F32)[_t5_bucket_table(n_tab)].T
    tq = np.arange(Q_BLOCK)[:, None]
    tk = np.arange(Q_BLOCK)[None, :]
    idx = np.stack([np.clip(dlt * Q_BLOCK + tq - tk, 0, n_tab - 1) for dlt in range(3)])
    toep = tab[:, idx]
    m = np.arange(LANES)[None, :] - 7
    didx = np.clip(CMP_STRIDE * m + tq - (CMP_BLOCK - 1), 0, n_tab - 1)
    near = np.broadcast_to(m <= 8, didx.shape)
    fd = jnp.where(near, tab[:, didx] - tab[:, n_tab - 1][:, None, None], 0.0)
    return toep, fd


def _nsa_layer(h2, batch, seq, norm_g, w_in, pos_k, pos_v, w1_k, w2_k, w1_v, w2_v, w_out, rel_bias):
    t, d = h2.shape
    hn = _rmsnorm(h2, norm_g, MXU_DTYPE)
    n_qkv = Q_COLS + N_KV * KV_COLS
    proj = _matmul(hn, w_in[:, :n_qkv].astype(MXU_DTYPE), MXU_DTYPE, name="nsa_qkv")
    wg = w_in[:, n_qkv:].reshape(d, 3, KV_GROUPS, HPG).transpose(0, 2, 1, 3).reshape(d, KV_GROUPS, 3 * HPG)
    wg = jnp.pad(wg, ((0, 0), (0, 0), (0, LANES - 3 * HPG))).reshape(d, KV_GROUPS * LANES)
    gates = _matmul(hn, wg.astype(MXU_DTYPE), F32, act="sigmoid", name="nsa_gates")
    proj3 = proj.reshape(batch, seq, n_qkv)
    kvc = _nsa_compress(proj3, jnp.stack([pos_k, pos_v]).astype(F32),
                        jnp.stack([w1_k, w1_v]).astype(MXU_DTYPE), jnp.stack([w2_k, w2_v]).astype(MXU_DTYPE))
    toep, fd = _nsa_bias_tables(rel_bias)
    o = _nsa_attention(proj3, kvc, gates.reshape(batch, seq, KV_GROUPS * LANES), toep, fd)
    return _matmul(o.reshape(t, Q_COLS), w_out.astype(MXU_DTYPE), F32, res=h2, name="nsa_out")


def _peer_scores_kernel(x_ref, wq_ref, sk_ref, st_ref):
    q = _dot(x_ref[...], wq_ref[...]).astype(MXU_DTYPE)
    half = PEER_DKEY // 2
    for hp in range(2 * PEER_HEADS):
        st_ref[hp] = _dot_nt(sk_ref[hp], q[:, hp * half:(hp + 1) * half])


def _peer_scores(xn, w_q, sub_keys, tm=256):
    t, d = xn.shape
    nhp = 2 * PEER_HEADS
    half = PEER_DKEY // 2
    return pl.pallas_call(
        _peer_scores_kernel,
        grid=(t // tm,),
        in_specs=[
            pl.BlockSpec((tm, d), lambda i: (i, 0)),
            pl.BlockSpec((d, nhp * half), lambda i: (0, 0)),
            pl.BlockSpec((nhp, PEER_NKEYS, half), lambda i: (0, 0, 0)),
        ],
        out_specs=pl.BlockSpec((nhp, PEER_NKEYS, tm), lambda i: (0, 0, i)),
        out_shape=jax.ShapeDtypeStruct((nhp, PEER_NKEYS, t), F32),
        compiler_params=_cparams("arbitrary"),
        name="peer_scores",
    )(xn, w_q, sub_keys)


def _peer_topk_kernel(st_ref, idx_ref, gate_ref, val_ref, pos_ref, cand_ref, cidx_ref, gt_ref, it_ref):
    tt = st_ref.shape[-1]
    k = PEER_TOPK

    def extract(a, payload, val_out, idx_out, base):
        n = a.shape[0]
        rows = lax.broadcasted_iota(jnp.int32, (n, tt), 0).astype(F32)
        for r in range(k):
            m = jnp.max(a, axis=0, keepdims=True)
            first = jnp.min(jnp.where(a == m, rows, float(n)), axis=0, keepdims=True)
            hit = rows == first
            val_out[base + r:base + r + 1, :] = m
            if payload is None:
                idx_out[base + r:base + r + 1, :] = first
            else:
                idx_out[base + r:base + r + 1, :] = jnp.sum(jnp.where(hit, payload, 0.0), axis=0, keepdims=True)
            a = jnp.where(hit, -jnp.inf, a)

    for h in range(PEER_HEADS):
        extract(st_ref[2 * h], None, val_ref, pos_ref, 0)
        extract(st_ref[2 * h + 1], None, val_ref, pos_ref, k)
        s2 = val_ref[k:2 * k, :]
        i2 = pos_ref[k:2 * k, :]
        for a in range(k):
            cand_ref[a * k:(a + 1) * k, :] = val_ref[a:a + 1, :] + s2
            cidx_ref[a * k:(a + 1) * k, :] = pos_ref[a:a + 1, :] * float(PEER_NKEYS) + i2
        extract(cand_ref[...], cidx_ref[...], gt_ref, it_ref, h * k)
        top = gt_ref[h * k:(h + 1) * k, :]
        e = jnp.exp(top - top[0:1, :])
        gt_ref[h * k:(h + 1) * k, :] = e / jnp.sum(e, axis=0, keepdims=True)
    gate_ref[...] = gt_ref[...].T
    idx_ref[...] = it_ref[...].T.astype(jnp.int32)


def _peer_topk(st, tt=128):
    nhp, nk, t = st.shape
    k = PEER_TOPK
    slots = PEER_HEADS * k
    return pl.pallas_call(
        _peer_topk_kernel,
        grid=(t // tt,),
        in_specs=[pl.BlockSpec((nhp, nk, tt), lambda i: (0, 0, i))],
        out_specs=[pl.BlockSpec((tt, slots), lambda i: (i, 0)), pl.BlockSpec((tt, slots), lambda i: (i, 0))],
        out_shape=[jax.ShapeDtypeStruct((t, slots), jnp.int32), jax.ShapeDtypeStruct((t, slots), F32)],
        scratch_shapes=[pltpu.VMEM((2 * k, tt), F32), pltpu.VMEM((2 * k, tt), F32),
                        pltpu.VMEM((k * k, tt), F32), pltpu.VMEM((k * k, tt), F32),
                        pltpu.VMEM((slots, tt), F32), pltpu.VMEM((slots, tt), F32)],
        compiler_params=_cparams("arbitrary"),
        name="peer_topk",
    )(st)


PEER_SLOTS = 3


def _peer_expert_kernel(idx_ref, x_ref, gate_ref, h_ref, u_hbm, v_hbm, o_ref, xf_ref, ubuf, vbuf, usem, vsem):
    tb = x_ref.shape[0]
    nk = gate_ref.shape[1]

    def row_copies(t, k, slot):
        e = idx_ref[t, k]
        return (pltpu.make_async_copy(u_hbm.at[pl.ds(e, 1), :], ubuf.at[slot, pl.ds(k, 1), :], usem.at[slot]),
                pltpu.make_async_copy(v_hbm.at[pl.ds(e, 1), :], vbuf.at[slot, pl.ds(k, 1), :], vsem.at[slot]))

    def issue(t, slot):
        def body(k, c):
            cu, cv = row_copies(t, k, slot)
            cu.start()
            cv.start()
            return c
        lax.fori_loop(0, nk, body, 0, unroll=8)

    def wait(slot):
        pltpu.make_async_copy(u_hbm.at[pl.ds(0, nk), :], ubuf.at[slot], usem.at[slot]).wait()
        pltpu.make_async_copy(v_hbm.at[pl.ds(0, nk), :], vbuf.at[slot], vsem.at[slot]).wait()

    for t0 in range(PEER_SLOTS - 1):
        issue(t0, t0)
    xf_ref[...] = x_ref[...].astype(F32)

    def token(t, c):
        slot = lax.rem(t, PEER_SLOTS)

        @pl.when(t + PEER_SLOTS - 1 < tb)
        def _():
            issue(t + PEER_SLOTS - 1, lax.rem(t + PEER_SLOTS - 1, PEER_SLOTS))

        wait(slot)
        x8 = jnp.broadcast_to(xf_ref[pl.ds(t, 1), :], (SUBLANES, x_ref.shape[1])).astype(MXU_DTYPE)
        act = jax.nn.gelu(_dot_nt(x8, ubuf[slot].astype(MXU_DTYPE)))
        w = (gate_ref[pl.ds(t, 1), :] * act).astype(MXU_DTYPE)
        out = _dot(w, vbuf[slot].astype(MXU_DTYPE))
        o_ref[pl.ds(t, 1), :] = h_ref[pl.ds(t, 1), :] + out[0:1, :]
        return c

    lax.fori_loop(0, tb, token, 0)


def _peer_experts(idx, xn, gate, h2, u_emb, v_emb, tb=64):
    t, d = xn.shape
    nk = idx.shape[1]
    return pl.pallas_call(
        _peer_expert_kernel,
        grid=(t // tb,),
        in_specs=[
            pl.BlockSpec((tb, nk), lambda i: (i, 0), memory_space=pltpu.SMEM),
            pl.BlockSpec((tb, d), lambda i: (i, 0)),
            pl.BlockSpec((tb, nk), lambda i: (i, 0)),
            pl.BlockSpec((tb, d), lambda i: (i, 0)),
            pl.BlockSpec(memory_space=pl.ANY),
            pl.BlockSpec(memory_space=pl.ANY),
        ],
        out_specs=pl.BlockSpec((tb, d), lambda i: (i, 0)),
        out_shape=jax.ShapeDtypeStruct((t, d), F32),
        scratch_shapes=[pltpu.VMEM((tb, d), F32),
                        pltpu.VMEM((PEER_SLOTS, nk, d), F32), pltpu.VMEM((PEER_SLOTS, nk, d), F32),
                        pltpu.SemaphoreType.DMA((PEER_SLOTS,)), pltpu.SemaphoreType.DMA((PEER_SLOTS,))],
        compiler_params=_cparams("arbitrary"),
        name="peer_experts",
    )(idx, xn, gate, h2, u_emb, v_emb)


def _peer_layer(h2, norm_g, w_q, sub_keys, u_emb, v_emb):
    xn = _rmsnorm(h2, norm_g, MXU_DTYPE)
    half = PEER_DKEY // 2
    st = _peer_scores(xn, w_q.astype(MXU_DTYPE),
                      sub_keys.reshape(2 * PEER_HEADS, PEER_NKEYS, half).astype(MXU_DTYPE))
    idx, gate = _peer_topk(st)
    return _peer_experts(idx, xn, gate, h2, u_emb.astype(F32), v_emb.astype(F32))


def kernel(x, rel_bias, mix_norm, ffn_norm, final_norm, conv_w_in, conv_kernel, conv_w_out, nsa_w_in,
           nsa_cmp_pos_k, nsa_cmp_pos_v, nsa_cmp_w1_k, nsa_cmp_w2_k, nsa_cmp_w1_v, nsa_cmp_w2_v, nsa_w_out,
           peer_w_q, peer_sub_keys, peer_u, peer_v):
    batch, seq, d = x.shape
    depth = mix_norm.shape[0]
    h = x.reshape(batch * seq, d).astype(F32)
    for i in range(depth):
        j = i // 2
        if i % 2 == 0:
            h = _shortconv_layer(h, seq, mix_norm[i], conv_w_in[j], conv_kernel[j], conv_w_out[j])
        else:
            h = _nsa_layer(h, batch, seq, mix_norm[i], nsa_w_in[j], nsa_cmp_pos_k[j], nsa_cmp_pos_v[j],
                           nsa_cmp_w1_k[j], nsa_cmp_w2_k[j], nsa_cmp_w1_v[j], nsa_cmp_w2_v[j], nsa_w_out[j],
                           rel_bias)
        h = _peer_layer(h, ffn_norm[i], peer_w_q[i], peer_sub_keys[i], peer_u[i], peer_v[i])
    return _rmsnorm(h, final_norm, x.dtype).reshape(batch, seq, d)
```
